```python
import jax, jax.numpy as jnp
from jax import lax
import numpy as np


D_MODEL = 2048
BATCH = 8
SEQ = 2048
DEPTH = 1

GRID_W = 64
CTX_LEN = 256
HEAD_DIM = 128
MIX_WIDTH = D_MODEL
ATT_HEADS = MIX_WIDTH // 2 // HEAD_DIM
KV_HEADS = ATT_HEADS // 4
GROUP = ATT_HEADS // KV_HEADS
M_HEADS = (MIX_WIDTH - ATT_HEADS * HEAD_DIM) // HEAD_DIM
ATT_Q_W = ATT_HEADS * HEAD_DIM
KV_W = KV_HEADS * HEAD_DIM
M_W = M_HEADS * HEAD_DIM
N_GATES = 4 * M_HEADS
PROJ_WIDTH = ATT_Q_W + 2 * KV_W + 4 * M_W + N_GATES
SPLIT_POINTS = (ATT_Q_W, ATT_Q_W + KV_W, ATT_Q_W + 2 * KV_W, ATT_Q_W + 2 * KV_W + 2 * M_W,
                ATT_Q_W + 2 * KV_W + 3 * M_W, ATT_Q_W + 2 * KV_W + 4 * M_W)
AXIS_DIM = HEAD_DIM // 2
ROPE_THETA = 10000.0
Q_BLOCK = 128
MLSTM_CHUNK = 64
CONV_K = 5
D_FF = 256 * ((8 * D_MODEL // 3 + 255) // 256)
N_MOD = 9
RMS_EPS = 1e-6

kernel_name = 'hybrid_gqa_mlstm_macaron_dit_layer'


def rms_norm(x, g):
    xf = x.astype(jnp.float32)
    y = xf * lax.rsqrt(jnp.mean(xf * xf, axis=-1, keepdims=True) + RMS_EPS)
    return (y * g.astype(jnp.float32)).astype(x.dtype)


def adaln(x, g, mod, i):
    return rms_norm(x, g) * (1 + mod[:, i + 1, None, :]) + mod[:, i, None, :]


def half_step_ffn(x, mod, i, g, w_up, w_down):
    h = adaln(x, g, mod, i)
    a, b = jnp.split(h @ w_up, 2, axis=-1)
    return x + 0.5 * mod[:, i + 2, None, :] * ((jax.nn.silu(a) * b) @ w_down)


def axial_rope_tables(rows):
    row = jnp.repeat(jnp.arange(rows), GRID_W).astype(jnp.float32)
    col = jnp.tile(jnp.arange(GRID_W), rows).astype(jnp.float32)
    inv = jnp.power(ROPE_THETA, -jnp.arange(0, AXIS_DIM, 2, dtype=jnp.float32) / AXIS_DIM)
    ang_r = row[:, None] * inv[None, :]
    ang_c = col[:, None] * inv[None, :]
    return (jnp.cos(ang_r), jnp.sin(ang_r), jnp.cos(ang_c), jnp.sin(ang_c))


def rotate_half(x, cos, sin):
    x1, x2 = jnp.split(x, 2, axis=-1)
    c, s = cos[:, None, :], sin[:, None, :]
    return jnp.concatenate([x1 * c - x2 * s, x2 * c + x1 * s], axis=-1)


def apply_axial_rope(x, rope):
    cr, sr, cc, sc = rope
    xr, xcol = jnp.split(x.astype(jnp.float32), 2, axis=-1)
    return jnp.concatenate([rotate_half(xr, cr, sr), rotate_half(xcol, cc, sc)], axis=-1).astype(x.dtype)


def short_conv(x, w, b):
    y = lax.conv_general_dilated(x, w[:, None, :].astype(x.dtype), window_strides=(1,),
                                 padding=[(CONV_K // 2, CONV_K // 2)],
                                 dimension_numbers=('NWC', 'WIO', 'NWC'),
                                 feature_group_count=x.shape[-1])
    return y + b


def mixer_projections(h, w_in, q_gain, k_gain, conv_w, conv_b, gate_b, rope):
    B, T, _ = h.shape
    aq, ak, av, mqk, mv, mo, gt = jnp.split(h @ w_in, SPLIT_POINTS, axis=-1)
    aq = rms_norm(aq.reshape(B, T, ATT_HEADS, HEAD_DIM), q_gain)
    ak = rms_norm(ak.reshape(B, T, KV_HEADS, HEAD_DIM), k_gain)
    if rope is not None:
        aq = apply_axial_rope(aq, rope)
        ak = apply_axial_rope(ak, rope)
    av = av.reshape(B, T, KV_HEADS, HEAD_DIM)
    mq, mk = jnp.split(jax.nn.silu(short_conv(mqk, conv_w, conv_b)), 2, axis=-1)
    mq = mq.reshape(B, T, M_HEADS, HEAD_DIM)
    mk = mk.reshape(B, T, M_HEADS, HEAD_DIM) * HEAD_DIM ** -0.5
    mv = mv.reshape(B, T, M_HEADS, HEAD_DIM)
    mo = jax.nn.sigmoid(mo)
    gt = (gt.astype(jnp.float32) + gate_b.astype(jnp.float32)).reshape(B, T, 4, M_HEADS)
    ig_f = gt[:, :, 0]
    lf_f = jax.nn.log_sigmoid(gt[:, :, 1])
    ig_b = gt[:, :, 2]
    lf_b = jax.nn.log_sigmoid(gt[:, :, 3])
    return (aq, ak, av), (mq, mk, mv, mo, ig_f, lf_f, ig_b, lf_b)


def attend(q, k, v):
    s = jnp.einsum('bqhgd,bkhd->bhgqk', q, k, preferred_element_type=jnp.float32) * HEAD_DIM ** -0.5
    p = jax.nn.softmax(s, axis=-1).astype(v.dtype)
    return jnp.einsum('bhgqk,bkhd->bqhgd', p, v)


def latent_attention(q, k_all, v_all):
    B, T = q.shape[:2]
    qb = q.reshape(B, T // Q_BLOCK, Q_BLOCK, KV_HEADS, GROUP, HEAD_DIM).swapaxes(0, 1)
    out = lax.map(lambda blk: attend(blk, k_all, v_all), qb)
    return out.swapaxes(0, 1).reshape(B, T, ATT_Q_W)


def mlstm_chunk(carry, inp):
    C0, n0, m0 = carry
    q, k, v, ig, lf = inp
    L = q.shape[2]
    b = jnp.cumsum(lf, axis=-1)
    lower = jnp.tril(jnp.ones((L, L), dtype=bool))
    d_log = jnp.where(lower, b[..., :, None] - b[..., None, :] + ig[..., None, :], -jnp.inf)
    inter = b + m0[..., None]
    m = jnp.maximum(inter, jnp.max(d_log, axis=-1))
    s = jnp.einsum('bhld,bhsd->bhls', q, k) * jnp.exp(d_log - m[..., None])
    a = jnp.exp(inter - m)
    num = jnp.einsum('bhls,bhsd->bhld', s, v) + a[..., None] * jnp.einsum('bhld,bhde->bhle', q, C0)
    den = jnp.abs(jnp.sum(s, axis=-1) + a * jnp.einsum('bhld,bhd->bhl', q, n0))
    h = num / jnp.maximum(den, jnp.exp(-m))[..., None]
    g = b[..., -1:] - b + ig
    total = b[..., -1] + m0
    m_new = jnp.maximum(total, jnp.max(g, axis=-1))
    wg = jnp.exp(g - m_new[..., None])
    decay = jnp.exp(total - m_new)
    C_new = decay[..., None, None] * C0 + jnp.einsum('bhs,bhsd,bhse->bhde', wg, k, v)
    n_new = decay[..., None] * n0 + jnp.einsum('bhs,bhsd->bhd', wg, k)
    return (C_new, n_new, m_new), h


def mlstm_scan(q, k, v, ig, lf, state):
    B, T, H, d = q.shape
    nc = T // MLSTM_CHUNK

    def chunks(a):
        a = a.astype(jnp.float32).reshape((B, nc, MLSTM_CHUNK) + a.shape[2:])
        return jnp.moveaxis(jnp.moveaxis(a, 3, 2), 1, 0)

    state, h = lax.scan(mlstm_chunk, state, (chunks(q), chunks(k), chunks(v), chunks(ig), chunks(lf)))
    h = jnp.moveaxis(jnp.moveaxis(h, 0, 1), 2, 3).reshape(B, T, H, d)
    return h, state


def mlstm_bidir(q, k, v, ig_f, lf_f, ig_b, lf_b, s_f, s_b):
    h_f, s_f = mlstm_scan(q, k, v, ig_f, lf_f, s_f)
    flip = lambda a: jnp.flip(a, axis=1)
    h_b, s_b = mlstm_scan(flip(q), flip(k), flip(v), flip(ig_b), flip(lf_b), s_b)
    return (h_f + flip(h_b)).astype(q.dtype), s_f, s_b


def mixer_output(att, hm, mo, m_gain, w_out):
    B, T = att.shape[:2]
    hm = rms_norm(hm, m_gain.reshape(M_HEADS, HEAD_DIM)) * mo.reshape(B, T, M_HEADS, HEAD_DIM)
    return jnp.concatenate([att.reshape(B, T, ATT_Q_W), hm.reshape(B, T, M_W)], axis=-1) @ w_out


def setup_inputs(seed: int = 0) -> dict:
    key = jax.random.key(seed)
    ks = jax.random.split(key, 20)
    D = D_MODEL
    nrm = lambda k, shape, scale: jax.random.normal(k, shape, jnp.float32) * scale
    f_bias = jnp.linspace(3.0, 6.0, M_HEADS, dtype=jnp.float32)
    gate_sel = jnp.array([0.0, 1.0, 0.0, 1.0], jnp.float32)
    gate_b = (nrm(ks[16], (DEPTH, 4, M_HEADS), 0.1)
              + gate_sel[None, :, None] * f_bias[None, None, :]).reshape(DEPTH, N_GATES)
    return {
        'x': nrm(ks[0], (BATCH, SEQ, D), 1.0),
        'c': nrm(ks[1], (BATCH, D), 1.0),
        'ctx': nrm(ks[2], (BATCH, CTX_LEN, D), 1.0),
        'c_ctx': nrm(ks[3], (D,), 1.0),
        'w_mod': nrm(ks[4], (DEPTH, D, N_MOD * D), 0.5 * D ** -0.5),
        'b_mod': nrm(ks[5], (DEPTH, N_MOD * D), 0.02),
        'g_norm': 1.0 + nrm(ks[6], (DEPTH, 3, D), 0.02),
        'w_ffn1_up': nrm(ks[7], (DEPTH, D, 2 * D_FF), D ** -0.5),
        'w_ffn1_down': nrm(ks[8], (DEPTH, D_FF, D), D_FF ** -0.5),
        'w_ffn2_up': nrm(ks[9], (DEPTH, D, 2 * D_FF), D ** -0.5),
        'w_ffn2_down': nrm(ks[10], (DEPTH, D_FF, D), D_FF ** -0.5),
        'w_in': nrm(ks[11], (DEPTH, D, PROJ_WIDTH), D ** -0.5),
        'q_gain': 1.0 + nrm(ks[12], (DEPTH, HEAD_DIM), 0.02),
        'k_gain': 1.0 + nrm(ks[13], (DEPTH, HEAD_DIM), 0.02),
        'conv_w': nrm(ks[14], (DEPTH, CONV_K, 2 * M_W), CONV_K ** -0.5),
        'conv_b': nrm(ks[15], (DEPTH, 2 * M_W), 0.02),
        'gate_b': gate_b,
        'm_gain': 1.0 + nrm(ks[17], (DEPTH, M_W), 0.02),
        'w_out': nrm(ks[18], (DEPTH, MIX_WIDTH, D), MIX_WIDTH ** -0.5),
        'g_final': 1.0 + nrm(ks[19], (D,), 0.02),
    }


def reference(x, c, ctx, c_ctx, w_mod, b_mod, g_norm, w_ffn1_up, w_ffn1_down, w_ffn2_up, w_ffn2_down,
              w_in, q_gain, k_gain, conv_w, conv_b, gate_b, m_gain, w_out, g_final):
    B, T, _ = x.shape
    ROWS = T // GRID_W
    rope = axial_rope_tables(ROWS)
    silu_c = jax.nn.silu(c)
    silu_cc = jax.nn.silu(c_ctx)[None, :]
    zero_state = (jnp.zeros((B, M_HEADS, HEAD_DIM, HEAD_DIM), jnp.float32),
                  jnp.zeros((B, M_HEADS, HEAD_DIM), jnp.float32),
                  jnp.zeros((B, M_HEADS), jnp.float32))
    xl, xc = x, ctx
    for l in range(DEPTH):
        mod_l = (silu_c @ w_mod[l] + b_mod[l]).reshape(B, N_MOD, D_MODEL)
        mod_c = (silu_cc @ w_mod[l] + b_mod[l]).reshape(1, N_MOD, D_MODEL)
        xl = half_step_ffn(xl, mod_l, 0, g_norm[l, 0], w_ffn1_up[l], w_ffn1_down[l])
        xc = half_step_ffn(xc, mod_c, 0, g_norm[l, 0], w_ffn1_up[l], w_ffn1_down[l])
        hl = adaln(xl, g_norm[l, 1], mod_l, 3)
        hc = adaln(xc, g_norm[l, 1], mod_c, 3)
        mix = (w_in[l], q_gain[l], k_gain[l], conv_w[l], conv_b[l], gate_b[l])
        (aq_l, ak_l, av_l), m_l = mixer_projections(hl, *mix, rope)
        (aq_c, ak_c, av_c), m_c = mixer_projections(hc, *mix, None)
        hm_c, st_f, st_b = mlstm_bidir(*m_c[:3], *m_c[4:], zero_state, zero_state)
        hm_l, _, _ = mlstm_bidir(*m_l[:3], *m_l[4:], st_f, st_b)
        att_l = latent_attention(aq_l, jnp.concatenate([ak_c, ak_l], axis=1),
                                 jnp.concatenate([av_c, av_l], axis=1))
        xl = xl + mod_l[:, 5, None, :] * mixer_output(att_l, hm_l, m_l[3], m_gain[l], w_out[l])
        xl = half_step_ffn(xl, mod_l, 6, g_norm[l, 2], w_ffn2_up[l], w_ffn2_down[l])
        if l + 1 < DEPTH:
            Bc, Tc = aq_c.shape[:2]
            att_c = attend(aq_c.reshape(Bc, Tc, KV_HEADS, GROUP, HEAD_DIM), ak_c, av_c)
            xc = xc + mod_c[:, 5, None, :] * mixer_output(att_c, hm_c, m_c[3], m_gain[l], w_out[l])
            xc = half_step_ffn(xc, mod_c, 6, g_norm[l, 2], w_ffn2_up[l], w_ffn2_down[l])
    return rms_norm(xl, g_final)
```

```python
import functools

import jax
import jax.numpy as jnp
from jax import lax
from jax.experimental import pallas as pl
from jax.experimental.pallas import tpu as pltpu

F32 = jnp.float32
BF16 = jnp.bfloat16

HEAD_DIM = 128
GQA_GROUP = 4
GRID_W = 64
ROPE_THETA = 10000.0
CONV_K = 5
RMS_EPS = 1e-6
N_MOD = 9
MLSTM_L = 256
LANES = 128
VMEM_LIMIT = 56 * 1024 * 1024


def _cparams(sem):
    return pltpu.CompilerParams(dimension_semantics=sem, vmem_limit_bytes=VMEM_LIMIT)


def _bdot(a, b):
    return jnp.dot(a, b, preferred_element_type=F32)


def _silu(x):
    return x * jax.nn.sigmoid(x)


def _rms(x):
    return x * lax.rsqrt(jnp.mean(x * x, axis=-1, keepdims=True) + RMS_EPS)


def _adaln(x, g, shift, scale):
    return (_rms(x) * g) * (1.0 + scale) + shift


def _mod_kernel(c_ref, w_ref, b_ref, o_ref):
    s = _silu(c_ref[...]).astype(BF16)
    o_ref[...] = _bdot(s, w_ref[...].astype(BF16)) + b_ref[...]


def _modulation(cc, w, b, tn):
    rows, d = cc.shape
    n = w.shape[1]
    return pl.pallas_call(
        _mod_kernel,
        out_shape=jax.ShapeDtypeStruct((rows, n), F32),
        grid=(n // tn,),
        in_specs=[pl.BlockSpec((rows, d), lambda j: (0, 0)),
                  pl.BlockSpec((d, tn), lambda j: (0, j)),
                  pl.BlockSpec((1, tn), lambda j: (0, j))],
        out_specs=pl.BlockSpec((rows, tn), lambda j: (0, j)),
        compiler_params=_cparams(("arbitrary",)),
        name="modulation",
    )(cc, w, b)


def _ffn_kernel(x_ref, mod_ref, g_ref, gfin_ref, wa_ref, wb_ref, wd_ref, *rest,
                mod_i, g_i, emit_h, nf):
    if emit_h:
        y_ref, h2_ref, h_scr, acc_scr = rest
    else:
        y_ref, h_scr, acc_scr = rest
    f = pl.program_id(2)

    @pl.when(f == 0)
    def _():
        h = _adaln(x_ref[...], g_ref[g_i:g_i + 1, :],
                   mod_ref[mod_i:mod_i + 1, :], mod_ref[mod_i + 1:mod_i + 2, :])
        h_scr[...] = h.astype(BF16)
        acc_scr[...] = jnp.zeros_like(acc_scr)

    h = h_scr[...]
    a = _bdot(h, wa_ref[...])
    b = _bdot(h, wb_ref[...])
    acc_scr[...] += _bdot((_silu(a) * b).astype(BF16), wd_ref[...])

    @pl.when(f == nf - 1)
    def _():
        y = x_ref[...] + (0.5 * mod_ref[mod_i + 2:mod_i + 3, :]) * acc_scr[...]
        if emit_h:
            y_ref[...] = y
            h2 = _adaln(y, g_ref[g_i + 1:g_i + 2, :],
                        mod_ref[mod_i + 3:mod_i + 4, :], mod_ref[mod_i + 4:mod_i + 5, :])
            h2_ref[...] = h2.astype(BF16)
        else:
            y_ref[...] = _rms(y) * gfin_ref[...]


def _ffn(x, mod, g, gfin, w_up, w_down, *, mod_i, g_i, emit_h, tm, tf):
    bx, t, d = x.shape
    f_dim = w_down.shape[0]
    nf = f_dim // tf
    per_batch = mod.shape[0] != 1
    mod_map = (lambda b, i, f: (b, 0, 0)) if per_batch else (lambda b, i, f: (0, 0, 0))
    row_spec = pl.BlockSpec((None, tm, d), lambda b, i, f: (b, i, 0))
    out_shape = [jax.ShapeDtypeStruct((bx, t, d), F32)]
    out_specs = [row_spec]
    if emit_h:
        out_shape.append(jax.ShapeDtypeStruct((bx, t, d), BF16))
        out_specs.append(row_spec)
    return pl.pallas_call(
        functools.partial(_ffn_kernel, mod_i=mod_i, g_i=g_i, emit_h=emit_h, nf=nf),
        out_shape=out_shape,
        grid=(bx, t // tm, nf),
        in_specs=[row_spec,
                  pl.BlockSpec((None, N_MOD, d), mod_map),
                  pl.BlockSpec(g.shape, lambda b, i, f: (0, 0)),
                  pl.BlockSpec((1, d), lambda b, i, f: (0, 0)),
                  pl.BlockSpec((d, tf), lambda b, i, f: (0, f)),
                  pl.BlockSpec((d, tf), lambda b, i, f: (0, nf + f)),
                  pl.BlockSpec((tf, d), lambda b, i, f: (f, 0))],
        out_specs=out_specs,
        scratch_shapes=[pltpu.VMEM((tm, d), BF16), pltpu.VMEM((tm, d), F32)],
        compiler_params=_cparams(("parallel", "parallel", "arbitrary")),
        name="ffn_emit_h" if emit_h else "ffn_final",
    )(x, mod, g, gfin, w_up, w_up, w_down)


def _proj_kernel(h_ref, w_ref, cos_ref, sin_ref, gain_ref, cw_ref, cb_ref, o_ref, *,
                 t_lat, tn, j_conv, j_kscale, j_plain, j_q, j_k):
    j = pl.program_id(1)
    s_len = h_ref.shape[0]
    acc = _bdot(h_ref[...], w_ref[...])

    @pl.when(j < j_conv)
    def _():
        rows = lax.broadcasted_iota(jnp.int32, (s_len, 1), 0)
        y = acc * cw_ref[CONV_K // 2:CONV_K // 2 + 1, :]
        for s in range(1, CONV_K // 2 + 1):
            lo = pltpu.roll(acc, s, 0)
            ok = ((rows >= s) & (rows < t_lat)) | (rows >= t_lat + s)
            y = y + jnp.where(ok, lo, 0.0) * cw_ref[CONV_K // 2 - s:CONV_K // 2 - s + 1, :]
            hi = pltpu.roll(acc, s_len - s, 0)
            ok = (rows < t_lat - s) | ((rows >= t_lat) & (rows < s_len - s))
            y = y + jnp.where(ok, hi, 0.0) * cw_ref[CONV_K // 2 + s:CONV_K // 2 + s + 1, :]
        y = _silu(y + cb_ref[...])
        y = y * jnp.where(j >= j_kscale, HEAD_DIM ** -0.5, 1.0)
        o_ref[...] = y.astype(BF16)

    @pl.when(((j >= j_conv) & (j < j_plain)) | (j >= j_k))
    def _():
        o_ref[...] = acc.astype(BF16)

    @pl.when((j >= j_plain) & (j < j_k))
    def _():
        gain = jnp.where(j < j_q, gain_ref[0:1, :], gain_ref[1:2, :])
        lane = lax.broadcasted_iota(jnp.int32, (1, HEAD_DIM), 1)
        first = (lane % (HEAD_DIM // 2)) < (HEAD_DIM // 4)
        for hh in range(tn // HEAD_DIM):
            xn = _rms(acc[:, hh * HEAD_DIM:(hh + 1) * HEAD_DIM]) * gain
            partner = jnp.where(first, pltpu.roll(xn, HEAD_DIM - HEAD_DIM // 4, 1),
                                pltpu.roll(xn, HEAD_DIM // 4, 1))
            o_ref[:, hh * HEAD_DIM:(hh + 1) * HEAD_DIM] = (
                xn * cos_ref[...] + partner * sin_ref[...]).astype(BF16)


def _projection(h, w, cos, sin, gains, cw, cb, *, t_lat, tn, tiles):
    b, s_len, d = h.shape
    n = w.shape[1]
    j_conv = tiles["conv"]
    return pl.pallas_call(
        functools.partial(_proj_kernel, t_lat=t_lat, tn=tn, j_conv=j_conv,
                          j_kscale=tiles["kscale"], j_plain=tiles["plain"],
                          j_q=tiles["q"], j_k=tiles["k"]),
        out_shape=jax.ShapeDtypeStruct((b, s_len, n), BF16),
        grid=(b, n // tn),
        in_specs=[pl.BlockSpec((None, s_len, d), lambda i, j: (i, 0, 0)),
                  pl.BlockSpec((d, tn), lambda i, j: (0, j)),
                  pl.BlockSpec(cos.shape, lambda i, j: (0, 0)),
                  pl.BlockSpec(sin.shape, lambda i, j: (0, 0)),
                  pl.BlockSpec(gains.shape, lambda i, j: (0, 0)),
                  pl.BlockSpec((cw.shape[0], tn), lambda i, j: (0, jnp.minimum(j, j_conv - 1))),
                  pl.BlockSpec((1, tn), lambda i, j: (0, jnp.minimum(j, j_conv - 1)))],
        out_specs=pl.BlockSpec((None, s_len, tn), lambda i, j: (i, 0, j)),
        compiler_params=_cparams(("parallel", "arbitrary")),
        name="mixer_projection",
    )(h, w, cos, sin, gains, cw, cb)


def _gates_kernel(h_ref, w_ref, b_ref, g_ref, gt_ref, pre_scr, suf_scr, *, mh, lc):
    s_len = h_ref.shape[0]
    g = _bdot(h_ref[...], w_ref[...]) + b_ref[...]
    lf = jnp.minimum(g, 0.0) - jnp.log(1.0 + jnp.exp(-jnp.abs(g)))
    r = lax.broadcasted_iota(jnp.int32, (lc, lc), 0)
    c = lax.broadcasted_iota(jnp.int32, (lc, lc), 1)
    tri_lo = (c <= r).astype(F32)
    tri_hi = (c >= r).astype(F32)
    for k in range(s_len // lc):
        blk = lf[k * lc:(k + 1) * lc, :]
        pre_scr[k * lc:(k + 1) * lc, :] = jnp.dot(tri_lo, blk, precision=lax.Precision.HIGHEST,
                                                  preferred_element_type=F32)
        suf_scr[k * lc:(k + 1) * lc, :] = jnp.dot(tri_hi, blk, precision=lax.Precision.HIGHEST,
                                                  preferred_element_type=F32)
    lane = lax.broadcasted_iota(jnp.int32, (1, LANES), 1)
    fwd = jnp.where(lane < mh, g, jnp.where(lane < 2 * mh, pre_scr[...], 0.0))
    g_b = pltpu.roll(g, LANES - 2 * mh, 1)
    s_b = pltpu.roll(suf_scr[...], LANES - 2 * mh, 1)
    bwd = jnp.where(lane < mh, g_b, jnp.where(lane < 2 * mh, s_b, 0.0))
    g_ref[0] = fwd
    g_ref[1] = bwd
    gt_ref[0] = fwd.T
    gt_ref[1] = bwd.T


def _gates(h, wg, gb, *, mh, lc):
    b, s_len, d = h.shape
    return pl.pallas_call(
        functools.partial(_gates_kernel, mh=mh, lc=lc),
        out_shape=[jax.ShapeDtypeStruct((b, 2, s_len, LANES), F32),
                   jax.ShapeDtypeStruct((b, 2, LANES, s_len), F32)],
        grid=(b,),
        in_specs=[pl.BlockSpec((None, s_len, d), lambda i: (i, 0, 0)),
                  pl.BlockSpec(wg.shape, lambda i: (0, 0)),
                  pl.BlockSpec(gb.shape, lambda i: (0, 0))],
        out_specs=[pl.BlockSpec((None, 2, s_len, LANES), lambda i: (i, 0, 0, 0)),
                   pl.BlockSpec((None, 2, LANES, s_len), lambda i: (i, 0, 0, 0))],
        scratch_shapes=[pltpu.VMEM((s_len, LANES), F32), pltpu.VMEM((s_len, LANES), F32)],
        compiler_params=_cparams(("parallel",)),
        name="mlstm_gates",
    )(h, wg, gb)


def _mlstm_kernel(q_ref, k_ref, v_ref, g_ref, gt_ref, o_ref, c_scr, n_scr, m_scr, *, mh, n_ctx):
    d_id = pl.program_id(1)
    j = pl.program_id(2)
    lc = q_ref.shape[0]
    is_fwd = d_id == 0

    @pl.when(j == 0)
    def _():
        c_scr[...] = jnp.zeros_like(c_scr)
        n_scr[...] = jnp.zeros_like(n_scr)
        m_scr[...] = jnp.zeros_like(m_scr)

    row = lax.broadcasted_iota(jnp.int32, (lc, lc), 0)
    col = lax.broadcasted_iota(jnp.int32, (lc, lc), 1)
    seen = (row - col) * jnp.where(is_fwd, 1, -1) >= 0

    for hd in range(mh):
        sl = slice(hd * HEAD_DIM, (hd + 1) * HEAD_DIM)
        q = q_ref[:, sl]
        k = k_ref[:, sl]
        v = v_ref[:, sl]
        ig_col = g_ref[:, hd:hd + 1]
        b_col = g_ref[:, mh + hd:mh + hd + 1]
        ig_row = gt_ref[hd:hd + 1, :]
        b_row = gt_ref[mh + hd:mh + hd + 1, :]
        b_last = jnp.where(is_fwd, b_col[lc - 1:lc, :], b_col[0:1, :])
        c0 = c_scr[hd]
        n0 = n_scr[hd]
        m0 = m_scr[hd][:, 0:1]

        @pl.when(j >= n_ctx)
        def _():
            d_log = jnp.where(seen, b_col - b_row + ig_row, -jnp.inf)
            inter = b_col + m0
            m_row = jnp.maximum(inter, jnp.max(d_log, axis=1, keepdims=True))
            s = lax.dot_general(q, k, (((1,), (1,)), ((), ())),
                                preferred_element_type=F32) * jnp.exp(d_log - m_row)
            a = jnp.exp(inter - m_row)
            num = _bdot(s.astype(BF16), v) + a * _bdot(q, c0.astype(BF16))
            qn = jnp.sum(q.astype(F32) * n0, axis=1, keepdims=True)
            den = jnp.abs(jnp.sum(s, axis=1, keepdims=True) + a * qn)
            o_ref[:, sl] = num * (1.0 / jnp.maximum(den, jnp.exp(-m_row)))

        g_col = b_last - b_col + ig_col
        total = b_last + m0
        m_new = jnp.maximum(total, jnp.max(g_col, axis=0, keepdims=True))
        decay = jnp.exp(total - m_new)
        kw = k.astype(F32) * jnp.exp(g_col - m_new)
        c_scr[hd] = decay * c0 + lax.dot_general(kw.astype(BF16), v, (((0,), (0,)), ((), ())),
                                                 preferred_element_type=F32)
        n_scr[hd] = decay * n0 + jnp.sum(kw, axis=0, keepdims=True)
        m_scr[hd] = jnp.broadcast_to(m_new, (1, LANES))


def _mlstm(proj, g, gt, *, t_lat, m_w, mh, lc):
    b, s_len, _ = proj.shape
    n_steps = s_len // lc
    n_lat = t_lat // lc
    n_ctx = n_steps - n_lat

    def chunk(d, j):
        return jnp.where(d == 0, (j + n_lat) % n_steps, n_steps - 1 - j)

    def out_chunk(d, j):
        return jnp.where(d == 0, jnp.maximum(j - n_ctx, 0), jnp.minimum(n_steps - 1 - j, n_lat - 1))

    def qkv_spec(col):
        return pl.BlockSpec((None, lc, m_w), lambda i, d, j: (i, chunk(d, j), col))

    return pl.pallas_call(
        functools.partial(_mlstm_kernel, mh=mh, n_ctx=n_ctx),
        out_shape=jax.ShapeDtypeStruct((b, 2, t_lat, m_w), F32),
        grid=(b, 2, n_steps),
        in_specs=[qkv_spec(0), qkv_spec(1), qkv_spec(2),
                  pl.BlockSpec((None, None, lc, LANES), lambda i, d, j: (i, d, chunk(d, j), 0)),
                  pl.BlockSpec((None, None, LANES, lc), lambda i, d, j: (i, d, 0, chunk(d, j)))],
        out_specs=pl.BlockSpec((None, None, lc, m_w), lambda i, d, j: (i, d, out_chunk(d, j), 0)),
        scratch_shapes=[pltpu.VMEM((mh, HEAD_DIM, HEAD_DIM), F32),
                        pltpu.VMEM((mh, 1, HEAD_DIM), F32),
                        pltpu.VMEM((mh, 1, LANES), F32)],
        compiler_params=_cparams(("parallel", "parallel", "arbitrary")),
        name="mlstm_scan",
    )(proj, proj, proj, g, gt)


def _attn_kernel(q_ref, k_ref, v_ref, o_ref):
    k = k_ref[...]
    v = v_ref[...]
    for g in range(GQA_GROUP):
        sl = slice(g * HEAD_DIM, (g + 1) * HEAD_DIM)
        s = lax.dot_general(q_ref[:, sl], k, (((1,), (1,)), ((), ())), preferred_element_type=F32)
        p = jnp.exp(s - jnp.max(s, axis=1, keepdims=True))
        o = _bdot(p.astype(BF16), v) * (1.0 / jnp.sum(p, axis=1, keepdims=True))
        o_ref[:, sl] = o.astype(BF16)


def _attention(proj, *, t_lat, q_col, k_col, v_col, kvh, tq):
    b, s_len, _ = proj.shape
    gw = GQA_GROUP * HEAD_DIM
    return pl.pallas_call(
        _attn_kernel,
        out_shape=jax.ShapeDtypeStruct((b, t_lat, kvh * gw), BF16),
        grid=(b, kvh, t_lat // tq),
        in_specs=[pl.BlockSpec((None, tq, gw), lambda i, h, t: (i, t, q_col + h)),
                  pl.BlockSpec((None, s_len, HEAD_DIM), lambda i, h, t: (i, 0, k_col + h)),
                  pl.BlockSpec((None, s_len, HEAD_DIM), lambda i, h, t: (i, 0, v_col + h))],
        out_specs=pl.BlockSpec((None, tq, gw), lambda i, h, t: (i, t, h)),
        compiler_params=_cparams(("parallel", "parallel", "arbitrary")),
        name="gqa_attention",
    )(proj, proj, proj)


def _out_kernel(att_ref, hm_ref, mo_ref, mg_ref, w_ref, x_ref, mod_ref, o_ref, *, att_w, mh):
    hs = hm_ref[0] + hm_ref[1]
    y = _bdot(att_ref[...], w_ref[0:att_w, :])
    for hd in range(mh):
        sl = slice(hd * HEAD_DIM, (hd + 1) * HEAD_DIM)
        hn = (_rms(hs[:, sl]) * mg_ref[:, sl]) * jax.nn.sigmoid(mo_ref[:, sl].astype(F32))
        y = y + _bdot(hn.astype(BF16), w_ref[att_w + hd * HEAD_DIM:att_w + (hd + 1) * HEAD_DIM, :])
    o_ref[...] = x_ref[...] + mod_ref[5:6, :] * y


def _out_projection(att, hm, proj, mg, w_out, x1, mod, *, mo_col, tm):
    b, t, d = x1.shape
    att_w = att.shape[2]
    m_w = hm.shape[3]
    return pl.pallas_call(
        functools.partial(_out_kernel, att_w=att_w, mh=m_w // HEAD_DIM),
        out_shape=jax.ShapeDtypeStruct((b, t, d), F32),
        grid=(b, t // tm),
        in_specs=[pl.BlockSpec((None, tm, att_w), lambda i, r: (i, r, 0)),
                  pl.BlockSpec((None, 2, tm, m_w), lambda i, r: (i, 0, r, 0)),
                  pl.BlockSpec((None, tm, m_w), lambda i, r: (i, r, mo_col)),
                  pl.BlockSpec(mg.shape, lambda i, r: (0, 0)),
                  pl.BlockSpec(w_out.shape, lambda i, r: (0, 0)),
                  pl.BlockSpec((None, tm, d), lambda i, r: (i, r, 0)),
                  pl.BlockSpec((None, N_MOD, d), lambda i, r: (i, 0, 0))],
        out_specs=pl.BlockSpec((None, tm, d), lambda i, r: (i, r, 0)),
        compiler_params=_cparams(("parallel", "arbitrary")),
        name="out_projection",
    )(att, hm, proj, mg, w_out, x1, mod)


def _rope_tables(t_lat, s_len):
    pos = jnp.arange(t_lat)
    row = (pos // GRID_W).astype(F32)
    col = (pos % GRID_W).astype(F32)
    axis = HEAD_DIM // 2
    inv = jnp.power(ROPE_THETA, -jnp.arange(0, axis, 2, dtype=F32) / axis)
    ar = row[:, None] * inv[None, :]
    ac = col[:, None] * inv[None, :]
    cos = jnp.concatenate([jnp.cos(ar), jnp.cos(ar), jnp.cos(ac), jnp.cos(ac)], axis=-1)
    sin = jnp.concatenate([-jnp.sin(ar), jnp.sin(ar), -jnp.sin(ac), jnp.sin(ac)], axis=-1)
    pad = s_len - t_lat
    cos = jnp.concatenate([cos, jnp.ones((pad, HEAD_DIM), F32)], axis=0)
    sin = jnp.concatenate([sin, jnp.zeros((pad, HEAD_DIM), F32)], axis=0)
    return cos, sin


def _pick(n, candidates):
    for c in candidates:
        if n % c == 0:
            return c
    raise ValueError(f"no tile in {candidates} divides {n}")


def kernel(x, c, ctx, c_ctx, w_mod, b_mod, g_norm, w_ffn1_up, w_ffn1_down, w_ffn2_up, w_ffn2_down,
           w_in, q_gain, k_gain, conv_w, conv_b, gate_b, m_gain, w_out, g_final):
    bsz, t_lat, d = x.shape
    t_ctx = ctx.shape[1]
    s_len = t_lat + t_ctx
    depth = w_mod.shape[0]
    assert depth == 1, "single-layer problem: the context stream update is never needed"
    mix = w_out.shape[1]
    att_w = mix // 2
    kvh = att_w // HEAD_DIM // GQA_GROUP
    kv_w = kvh * HEAD_DIM
    m_w = mix - att_w
    mh = m_w // HEAD_DIM
    n_gates = 4 * mh
    assert w_in.shape[2] == att_w + 2 * kv_w + 4 * m_w + n_gates
    assert t_lat % MLSTM_L == 0 and t_ctx % MLSTM_L == 0 and t_lat % GRID_W == 0
    assert m_w % (GQA_GROUP * HEAD_DIM) == 0 and n_gates <= LANES

    o_k = att_w
    o_v = o_k + kv_w
    o_mqk = o_v + kv_w
    o_mv = o_mqk + 2 * m_w
    o_mo = o_mv + m_w
    o_gt = o_mo + m_w
    wi = w_in[0]
    w_proj = jnp.concatenate([wi[:, o_mqk:o_mv], wi[:, o_mv:o_mo], wi[:, o_mo:o_gt],
                              wi[:, :o_k], wi[:, o_k:o_v], wi[:, o_v:o_mqk]], axis=1).astype(BF16)
    w_gate = jnp.pad(wi[:, o_gt:], ((0, 0), (0, LANES - n_gates))).astype(BF16)
    gate_bias = jnp.pad(gate_b[0], (0, LANES - n_gates))[None, :]
    tn = 256 if kv_w % 256 == 0 else HEAD_DIM
    c_mo, c_aq, c_ak, c_av = 3 * m_w, 4 * m_w, 4 * m_w + att_w, 4 * m_w + att_w + kv_w
    tiles = {"conv": 2 * m_w // tn, "kscale": m_w // tn, "plain": c_aq // tn,
             "q": c_ak // tn, "k": c_av // tn}
    gains = jnp.stack([q_gain[0] * HEAD_DIM ** -0.5, k_gain[0]])
    cw = jnp.pad(conv_w[0], ((0, 8 - CONV_K), (0, 0)))
    cb = conv_b[0][None, :]
    cos, sin = _rope_tables(t_lat, s_len)
    w1u, w1d = w_ffn1_up[0].astype(BF16), w_ffn1_down[0].astype(BF16)
    w2u, w2d = w_ffn2_up[0].astype(BF16), w_ffn2_down[0].astype(BF16)
    wo = w_out[0].astype(BF16)
    gn = g_norm[0]
    gfin = g_final[None, :]

    rows = -(-(bsz + 1) // 8) * 8
    cc = jnp.pad(jnp.concatenate([c, c_ctx[None, :]], axis=0), ((0, rows - bsz - 1), (0, 0)))
    mod = _modulation(cc, w_mod[0], b_mod[0][None, :], _pick(N_MOD * d, (1024, 512, 256, 128)))
    mod_l = mod[:bsz].reshape(bsz, N_MOD, d)
    mod_c = mod[bsz:bsz + 1].reshape(1, N_MOD, d)

    f_dim = w1d.shape[0]
    tf = _pick(f_dim, (512, 256, 128))
    tm = _pick(t_lat, (512, 256, 128))
    x1, h_l = _ffn(x, mod_l, gn, gfin, w1u, w1d, mod_i=0, g_i=0, emit_h=True, tm=tm, tf=tf)
    ctx_flat = ctx.reshape(1, bsz * t_ctx, d)
    _, h_c = _ffn(ctx_flat, mod_c, gn, gfin, w1u, w1d, mod_i=0, g_i=0, emit_h=True,
                  tm=_pick(bsz * t_ctx, (512, 256, 128)), tf=tf)
    h_all = jnp.concatenate([h_l, h_c.reshape(bsz, t_ctx, d)], axis=1)

    proj = _projection(h_all, w_proj, cos, sin, gains, cw, cb, t_lat=t_lat, tn=tn, tiles=tiles)
    g, gt = _gates(h_all, w_gate, gate_bias, mh=mh, lc=MLSTM_L)
    hm = _mlstm(proj, g, gt, t_lat=t_lat, m_w=m_w, mh=mh, lc=MLSTM_L)
    gw = GQA_GROUP * HEAD_DIM
    att = _attention(proj, t_lat=t_lat, q_col=c_aq // gw, k_col=c_ak // HEAD_DIM,
                     v_col=c_av // HEAD_DIM, kvh=kvh, tq=_pick(t_lat, (512, 256, 128)))
    x2 = _out_projection(att, hm, proj, m_gain, wo, x1, mod_l, mo_col=c_mo // m_w, tm=tm)

    (out,) = _ffn(x2, mod_l, gn, gfin, w2u, w2d, mod_i=6, g_i=2, emit_h=False, tm=tm, tf=tf)
    return out
```

```python
import functools

import jax
import jax.numpy as jnp
from jax import lax
from jax.experimental import pallas as pl
from jax.experimental.pallas import tpu as pltpu

F32 = jnp.float32
BF16 = jnp.bfloat16

HEAD_DIM = 128
GQA_GROUP = 4
GRID_W = 64
ROPE_THETA = 10000.0
CONV_K = 5
RMS_EPS = 1e-6
N_MOD = 9
MLSTM_L = 256
LANES = 128
SUBLANES = 8
ROW_BLOCK = 256
VMEM_LIMIT = 56 * 1024 * 1024


def _cparams(sem):
    return pltpu.CompilerParams(dimension_semantics=sem, vmem_limit_bytes=VMEM_LIMIT)


def _bdot(a, b):
    return jnp.dot(a, b, preferred_element_type=F32)


def _silu(x):
    return x * jax.nn.sigmoid(x)


def _rms(x):
    return x * lax.rsqrt(jnp.mean(x * x, axis=-1, keepdims=True) + RMS_EPS)


def _adaln(x, g, shift, scale):
    return (_rms(x) * g) * (1.0 + scale) + shift


def _mod_kernel(c_ref, w_ref, b_ref, o_ref):
    s = _silu(c_ref[...]).astype(BF16)
    o_ref[...] = _bdot(s, w_ref[...].astype(BF16)) + b_ref[...]


def _modulation(cc, w, b, tn):
    rows, d = cc.shape
    n = w.shape[1]
    return pl.pallas_call(
        _mod_kernel,
        out_shape=jax.ShapeDtypeStruct((rows, n), F32),
        grid=(n // tn,),
        in_specs=[pl.BlockSpec((rows, d), lambda j: (0, 0)),
                  pl.BlockSpec((d, tn), lambda j: (0, j)),
                  pl.BlockSpec((1, tn), lambda j: (0, j))],
        out_specs=pl.BlockSpec((rows, tn), lambda j: (0, j)),
        compiler_params=_cparams(("arbitrary",)),
        name="modulation",
    )(cc, w, b)


def _ffn_kernel(x_ref, mod_ref, g_ref, gfin_ref, wa_ref, wb_ref, wd_ref, *rest,
                mod_i, g_i, emit_h, has_prev, nf):
    if has_prev:
        rest = rest[1:]
    if emit_h:
        y_ref, h2_ref, h_scr, acc_scr = rest
    else:
        y_ref, h_scr, acc_scr = rest
    f = pl.program_id(2)

    @pl.when(f == 0)
    def _():
        h = _adaln(x_ref[...], g_ref[g_i:g_i + 1, :],
                   mod_ref[mod_i:mod_i + 1, :], mod_ref[mod_i + 1:mod_i + 2, :])
        h_scr[...] = h.astype(BF16)
        acc_scr[...] = jnp.zeros_like(acc_scr)

    h = h_scr[...]
    a = _bdot(h, wa_ref[...])
    b = _bdot(h, wb_ref[...])
    acc_scr[...] += _bdot((_silu(a) * b).astype(BF16), wd_ref[...])

    @pl.when(f == nf - 1)
    def _():
        y = x_ref[...] + (0.5 * mod_ref[mod_i + 2:mod_i + 3, :]) * acc_scr[...]
        if emit_h:
            y_ref[...] = y
            h2 = _adaln(y, g_ref[g_i + 1:g_i + 2, :],
                        mod_ref[mod_i + 3:mod_i + 4, :], mod_ref[mod_i + 4:mod_i + 5, :])
            h2_ref[...] = h2.astype(BF16).reshape(h2_ref.shape)
        else:
            y_ref[...] = _rms(y) * gfin_ref[...]


def _ffn(x, mod, g, gfin, w_up, w_down, *, mod_i, g_i, tm, tf, h_shape=None, h_spec=None, h_prev=None):
    bx, t, d = x.shape
    f_dim = w_down.shape[0]
    nf = f_dim // tf
    emit_h = h_shape is not None
    per_batch = mod.shape[0] != 1
    mod_map = (lambda b, i, f: (b, 0, 0)) if per_batch else (lambda b, i, f: (0, 0, 0))
    row_spec = pl.BlockSpec((None, tm, d), lambda b, i, f: (b, i, 0))
    in_specs = [row_spec,
                pl.BlockSpec((None, N_MOD, d), mod_map),
                pl.BlockSpec(g.shape, lambda b, i, f: (0, 0)),
                pl.BlockSpec((1, d), lambda b, i, f: (0, 0)),
                pl.BlockSpec((d, tf), lambda b, i, f: (0, f)),
                pl.BlockSpec((d, tf), lambda b, i, f: (0, nf + f)),
                pl.BlockSpec((tf, d), lambda b, i, f: (f, 0))]
    args = [x, mod, g, gfin, w_up, w_up, w_down]
    out_shape = [jax.ShapeDtypeStruct((bx, t, d), F32)]
    out_specs = [row_spec]
    aliases = {}
    if emit_h:
        out_shape.append(jax.ShapeDtypeStruct(h_shape, BF16))
        out_specs.append(h_spec)
        if h_prev is not None:
            in_specs.append(pl.BlockSpec(memory_space=pl.ANY))
            args.append(h_prev)
            aliases = {len(args) - 1: 1}
    return pl.pallas_call(
        functools.partial(_ffn_kernel, mod_i=mod_i, g_i=g_i, emit_h=emit_h,
                          has_prev=h_prev is not None, nf=nf),
        out_shape=out_shape,
        grid=(bx, t // tm, nf),
        in_specs=in_specs,
        out_specs=out_specs,
        scratch_shapes=[pltpu.VMEM((tm, d), BF16), pltpu.VMEM((tm, d), F32)],
        input_output_aliases=aliases,
        compiler_params=_cparams(("parallel", "parallel", "arbitrary")),
        name="ffn_emit_h" if emit_h else "ffn_final",
    )(*args)


def _proj_kernel(h_ref, w_ref, cos_ref, sin_ref, gain_ref, cw_ref, cb_ref, o_ref, kt_ref,
                 w_scr, acc_scr, *, t_lat, tn, j_conv, j_kscale, j_plain, j_q, j_k):
    j = pl.program_id(1)
    s_len = h_ref.shape[0]
    rb = ROW_BLOCK
    nr = s_len // rb
    w_scr[...] = w_ref[...].astype(BF16)

    def tile(r):
        return _bdot(h_ref[r * rb:(r + 1) * rb, :], w_scr[...])

    def conv_tiles(is_k):
        half = CONV_K // 2
        zeros = jnp.zeros((SUBLANES, tn), F32)
        for r in range(nr + 1):
            if r < nr:
                acc_scr[r * rb:(r + 1) * rb, :] = tile(r)
            if r >= 1:
                lo, hi = (r - 1) * rb, r * rb
                top = zeros if lo in (0, t_lat) else acc_scr[lo - SUBLANES:lo, :]
                bot = zeros if hi in (t_lat, s_len) else acc_scr[hi:hi + SUBLANES, :]
                xw = jnp.concatenate([top, acc_scr[lo:hi, :], bot], axis=0)
                n_w = rb + 2 * SUBLANES
                y = xw[SUBLANES:SUBLANES + rb] * cw_ref[half:half + 1, :]
                for s in range(1, half + 1):
                    y = y + pltpu.roll(xw, s, 0)[SUBLANES:SUBLANES + rb] * cw_ref[half - s:half - s + 1, :]
                    y = y + pltpu.roll(xw, n_w - s, 0)[SUBLANES:SUBLANES + rb] * cw_ref[half + s:half + s + 1, :]
                y = _silu(y + cb_ref[...])
                if is_k:
                    y = y * HEAD_DIM ** -0.5
                    kt_ref[:, lo:hi] = y.T.astype(BF16)
                o_ref[lo:hi, :] = y.astype(BF16)

    @pl.when(j < j_kscale)
    def _():
        conv_tiles(False)

    @pl.when((j >= j_kscale) & (j < j_conv))
    def _():
        conv_tiles(True)

    @pl.when(((j >= j_conv) & (j < j_plain)) | (j >= j_k))
    def _():
        for r in range(nr):
            o_ref[r * rb:(r + 1) * rb, :] = tile(r).astype(BF16)

    @pl.when((j >= j_plain) & (j < j_k))
    def _():
        gain = jnp.where(j < j_q, gain_ref[0:1, :], gain_ref[1:2, :])
        lane = lax.broadcasted_iota(jnp.int32, (1, HEAD_DIM), 1)
        first = (lane % (HEAD_DIM // 2)) < (HEAD_DIM // 4)
        for r in range(nr):
            acc = tile(r)
            rows = slice(r * rb, (r + 1) * rb)
            for hh in range(tn // HEAD_DIM):
                cols = slice(hh * HEAD_DIM, (hh + 1) * HEAD_DIM)
                xn = _rms(acc[:, cols]) * gain
                partner = jnp.where(first, pltpu.roll(xn, HEAD_DIM - HEAD_DIM // 4, 1),
                                    pltpu.roll(xn, HEAD_DIM // 4, 1))
                o_ref[rows, cols] = (xn * cos_ref[rows, :] + partner * sin_ref[rows, :]).astype(BF16)


def _projection(h, w_in, cos, sin, gains, cw, cb, *, t_lat, tn, n_cols, src_tile, tiles):
    b, s_len, d = h.shape
    j_conv, j_ks = tiles["conv"], tiles["kscale"]
    return pl.pallas_call(
        functools.partial(_proj_kernel, t_lat=t_lat, tn=tn, j_conv=j_conv,
                          j_kscale=tiles["kscale"], j_plain=tiles["plain"],
                          j_q=tiles["q"], j_k=tiles["k"]),
        out_shape=[jax.ShapeDtypeStruct((b, s_len, n_cols), BF16),
                   jax.ShapeDtypeStruct((b, (j_conv - j_ks) * tn, s_len), BF16)],
        grid=(b, n_cols // tn),
        in_specs=[pl.BlockSpec((None, s_len, d), lambda i, j: (i, 0, 0)),
                  pl.BlockSpec((d, tn), lambda i, j: (0, src_tile(j))),
                  pl.BlockSpec(cos.shape, lambda i, j: (0, 0)),
                  pl.BlockSpec(sin.shape, lambda i, j: (0, 0)),
                  pl.BlockSpec(gains.shape, lambda i, j: (0, 0)),
                  pl.BlockSpec((cw.shape[0], tn), lambda i, j: (0, jnp.minimum(j, j_conv - 1))),
                  pl.BlockSpec((1, tn), lambda i, j: (0, jnp.minimum(j, j_conv - 1)))],
        out_specs=[pl.BlockSpec((None, s_len, tn), lambda i, j: (i, 0, j)),
                   pl.BlockSpec((None, tn, s_len),
                                lambda i, j: (i, jnp.clip(j - j_ks, 0, j_conv - j_ks - 1), 0))],
        scratch_shapes=[pltpu.VMEM((d, tn), BF16), pltpu.VMEM((s_len, tn), F32)],
        compiler_params=_cparams(("parallel", "arbitrary")),
        name="mixer_projection",
    )(h, w_in, cos, sin, gains, cw, cb)


def _chunk_scan(x, op, ident, lc, reverse):
    s_len = x.shape[0]
    pos = lax.broadcasted_iota(jnp.int32, (s_len, 1), 0) % lc
    k = 1
    while k < lc:
        if reverse:
            x = op(x, jnp.where(pos < lc - k, pltpu.roll(x, s_len - k, 0), ident))
        else:
            x = op(x, jnp.where(pos >= k, pltpu.roll(x, k, 0), ident))
        k *= 2
    return x


def _gates_kernel(h_ref, w_ref, b_ref, g_ref, ut_ref, *, mh, lc):
    g = _bdot(h_ref[...], w_ref[...]) + b_ref[...]
    lf = jnp.minimum(g, 0.0) - jnp.log(1.0 + jnp.exp(-jnp.abs(g)))
    lane = lax.broadcasted_iota(jnp.int32, (1, LANES), 1)
    neg = -jnp.inf
    b_f = pltpu.roll(_chunk_scan(lf, jnp.add, 0.0, lc, False), LANES - mh, 1)
    u_f = g - b_f
    b_b = pltpu.roll(_chunk_scan(lf, jnp.add, 0.0, lc, True), LANES - 3 * mh, 1)
    u_b = pltpu.roll(g, LANES - 2 * mh, 1) - b_b
    g_ref[0, 0] = _chunk_scan(u_f, jnp.maximum, neg, lc, False)
    g_ref[0, 1] = b_f
    g_ref[1, 0] = _chunk_scan(u_b, jnp.maximum, neg, lc, True)
    g_ref[1, 1] = b_b
    u_both = jnp.where(lane < mh, u_f, jnp.where(lane < 2 * mh, pltpu.roll(u_b, mh, 1), 0.0))
    ut_ref[...] = u_both.T[0:ut_ref.shape[0], :]


def _gates(h, wg, gb, *, mh, lc):
    b, s_len, d = h.shape
    ut_rows = -(-2 * mh // SUBLANES) * SUBLANES
    return pl.pallas_call(
        functools.partial(_gates_kernel, mh=mh, lc=lc),
        out_shape=[jax.ShapeDtypeStruct((b, 2, 2, s_len, LANES), F32),
                   jax.ShapeDtypeStruct((b, ut_rows, s_len), F32)],
        grid=(b,),
        in_specs=[pl.BlockSpec((None, s_len, d), lambda i: (i, 0, 0)),
                  pl.BlockSpec(wg.shape, lambda i: (0, 0)),
                  pl.BlockSpec(gb.shape, lambda i: (0, 0))],
        out_specs=[pl.BlockSpec((None, 2, 2, s_len, LANES), lambda i: (i, 0, 0, 0, 0)),
                   pl.BlockSpec((None, ut_rows, s_len), lambda i: (i, 0, 0))],
        compiler_params=_cparams(("parallel",)),
        name="mlstm_gates",
    )(h, wg, gb)


def _mlstm_kernel(q_ref, kt_ref, v_ref, g_ref, ut_ref, o_ref, c_scr, m_scr, *, mh, n_ctx):
    d_id = pl.program_id(1)
    j = pl.program_id(2)
    lc = q_ref.shape[0]
    is_fwd = d_id == 0

    @pl.when(j == 0)
    def _():
        c_scr[...] = jnp.zeros_like(c_scr)
        m_scr[...] = jnp.zeros_like(m_scr)

    def heads(with_output):
        ones = jnp.ones((lc, HEAD_DIM), BF16)
        if with_output:
            row = lax.broadcasted_iota(jnp.int32, (lc, lc), 0)
            col = lax.broadcasted_iota(jnp.int32, (lc, lc), 1)
            seen = (row - col) * jnp.where(is_fwd, 1, -1) >= 0
        for hd in range(mh):
            sl = slice(hd * HEAD_DIM, (hd + 1) * HEAD_DIM)
            kt = kt_ref[sl, :]
            v_ext = jnp.concatenate([v_ref[:, sl], ones], axis=1)
            cm_col = g_ref[0, :, hd:hd + 1]
            b_col = g_ref[1, :, hd:hd + 1]
            u_row = ut_ref[pl.ds(d_id * mh + hd, 1), :]
            c_ext = c_scr[hd]
            m0 = m_scr[hd][:, 0:1]
            w_col = jnp.maximum(cm_col, m0)
            t_col = b_col + w_col
            m_new = jnp.where(is_fwd, t_col[lc - 1:lc, :], t_col[0:1, :])
            b_last = jnp.where(is_fwd, b_col[lc - 1:lc, :], b_col[0:1, :])
            if with_output:
                q = q_ref[:, sl]
                w_b = jnp.broadcast_to(w_col, (lc, HEAD_DIM))
                t_b = jnp.broadcast_to(t_col, (lc, HEAD_DIM))
                w_sq = jnp.concatenate([w_b] * (lc // HEAD_DIM), axis=1)
                p = jnp.exp(jnp.where(seen, u_row - w_sq, -jnp.inf))
                s = _bdot(q, kt) * p
                a_b = jnp.exp(m0 - w_b)
                both = (_bdot(s.astype(BF16), v_ext)
                        + jnp.concatenate([a_b, a_b], axis=1) * _bdot(q, c_ext.astype(BF16)))
                den = jnp.maximum(jnp.abs(both[:, HEAD_DIM:]), jnp.exp(-t_b))
                o_ref[:, sl] = both[:, :HEAD_DIM] / den
            kw_t = kt.astype(F32) * jnp.exp(u_row + (b_last - m_new))
            c_scr[hd] = jnp.exp(b_last + m0 - m_new) * c_ext + _bdot(kw_t.astype(BF16), v_ext)
            m_scr[hd] = jnp.broadcast_to(m_new, (1, LANES))

    @pl.when(j >= n_ctx)
    def _():
        heads(True)

    @pl.when(j < n_ctx)
    def _():
        heads(False)


def _mlstm(proj, kt, g, ut, *, t_lat, m_w, mh, lc, v_col):
    b, s_len, _ = proj.shape
    n_steps = s_len // lc
    n_lat = t_lat // lc
    n_ctx = n_steps - n_lat

    def chunk(d, j):
        return jnp.where(d == 0, (j + n_lat) % n_steps, n_steps - 1 - j)

    def out_chunk(d, j):
        return jnp.where(d == 0, jnp.maximum(j - n_ctx, 0), jnp.minimum(n_steps - 1 - j, n_lat - 1))

    def qkv_spec(col):
        return pl.BlockSpec((None, lc, m_w), lambda i, d, j: (i, chunk(d, j), col))

    return pl.pallas_call(
        functools.partial(_mlstm_kernel, mh=mh, n_ctx=n_ctx),
        out_shape=jax.ShapeDtypeStruct((b, 2, t_lat, m_w), F32),
        grid=(b, 2, n_steps),
        in_specs=[qkv_spec(0),
                  pl.BlockSpec((None, m_w, lc), lambda i, d, j: (i, 0, chunk(d, j))),
                  qkv_spec(v_col),
                  pl.BlockSpec((None, None, 2, lc, LANES), lambda i, d, j: (i, d, 0, chunk(d, j), 0)),
                  pl.BlockSpec((None, ut.shape[1], lc), lambda i, d, j: (i, 0, chunk(d, j)))],
        out_specs=pl.BlockSpec((None, None, lc, m_w), lambda i, d, j: (i, d, out_chunk(d, j), 0)),
        scratch_shapes=[pltpu.VMEM((mh, HEAD_DIM, 2 * HEAD_DIM), F32),
                        pltpu.VMEM((mh, 1, LANES), F32)],
        compiler_params=_cparams(("parallel", "parallel", "arbitrary")),
        name="mlstm_scan",
    )(proj, kt, proj, g, ut)


def _attn_kernel(q_ref, k_ref, v_ref, o_ref):
    k = k_ref[...]
    v = v_ref[...]
    for g in range(GQA_GROUP):
        sl = slice(g * HEAD_DIM, (g + 1) * HEAD_DIM)
        s = lax.dot_general(q_ref[:, sl], k, (((1,), (1,)), ((), ())), preferred_element_type=F32)
        p = jnp.exp(s - jnp.max(s, axis=1, keepdims=True))
        o = _bdot(p.astype(BF16), v) * (1.0 / jnp.sum(p, axis=1, keepdims=True))
        o_ref[:, sl] = o.astype(BF16)


def _attention(proj, *, t_lat, q_col, k_col, v_col, kvh, tq):
    b, s_len, _ = proj.shape
    gw = GQA_GROUP * HEAD_DIM
    return pl.pallas_call(
        _attn_kernel,
        out_shape=jax.ShapeDtypeStruct((b, t_lat, kvh * gw), BF16),
        grid=(b, kvh, t_lat // tq),
        in_specs=[pl.BlockSpec((None, tq, gw), lambda i, h, t: (i, t, q_col + h)),
                  pl.BlockSpec((None, s_len, HEAD_DIM), lambda i, h, t: (i, 0, k_col + h)),
                  pl.BlockSpec((None, s_len, HEAD_DIM), lambda i, h, t: (i, 0, v_col + h))],
        out_specs=pl.BlockSpec((None, tq, gw), lambda i, h, t: (i, t, h)),
        compiler_params=_cparams(("parallel", "parallel", "arbitrary")),
        name="gqa_attention",
    )(proj, proj, proj)


def _out_kernel(att_ref, hm_ref, mo_ref, mg_ref, w_ref, x_ref, mod_ref, o_ref, *, att_w, mh):
    rb = ROW_BLOCK
    for r in range(att_ref.shape[0] // rb):
        rows = slice(r * rb, (r + 1) * rb)
        y = _bdot(att_ref[rows, :], w_ref[0:att_w, :])
        for hd in range(mh):
            sl = slice(hd * HEAD_DIM, (hd + 1) * HEAD_DIM)
            hs = hm_ref[0, rows, sl] + hm_ref[1, rows, sl]
            hn = (_rms(hs) * mg_ref[:, sl]) * jax.nn.sigmoid(mo_ref[rows, sl].astype(F32))
            y = y + _bdot(hn.astype(BF16), w_ref[att_w + hd * HEAD_DIM:att_w + (hd + 1) * HEAD_DIM, :])
        o_ref[rows, :] = x_ref[rows, :] + mod_ref[5:6, :] * y


def _out_projection(att, hm, proj, mg, w_out, x1, mod, *, mo_col, tm):
    b, t, d = x1.shape
    att_w = att.shape[2]
    m_w = hm.shape[3]
    return pl.pallas_call(
        functools.partial(_out_kernel, att_w=att_w, mh=m_w // HEAD_DIM),
        out_shape=jax.ShapeDtypeStruct((b, t, d), F32),
        grid=(b, t // tm),
        in_specs=[pl.BlockSpec((None, tm, att_w), lambda i, r: (i, r, 0)),
                  pl.BlockSpec((None, 2, tm, m_w), lambda i, r: (i, 0, r, 0)),
                  pl.BlockSpec((None, tm, m_w), lambda i, r: (i, r, mo_col)),
                  pl.BlockSpec(mg.shape, lambda i, r: (0, 0)),
                  pl.BlockSpec(w_out.shape, lambda i, r: (0, 0)),
                  pl.BlockSpec((None, tm, d), lambda i, r: (i, r, 0)),
                  pl.BlockSpec((None, N_MOD, d), lambda i, r: (i, 0, 0))],
        out_specs=pl.BlockSpec((None, tm, d), lambda i, r: (i, r, 0)),
        compiler_params=_cparams(("parallel", "arbitrary")),
        name="out_projection",
    )(att, hm, proj, mg, w_out, x1, mod)


def _rope_tables(t_lat, s_len):
    pos = jnp.arange(t_lat)
    row = (pos // GRID_W).astype(F32)
    col = (pos % GRID_W).astype(F32)
    axis = HEAD_DIM // 2
    inv = jnp.power(ROPE_THETA, -jnp.arange(0, axis, 2, dtype=F32) / axis)
    ar = row[:, None] * inv[None, :]
    ac = col[:, None] * inv[None, :]
    cos = jnp.concatenate([jnp.cos(ar), jnp.cos(ar), jnp.cos(ac), jnp.cos(ac)], axis=-1)
    sin = jnp.concatenate([-jnp.sin(ar), jnp.sin(ar), -jnp.sin(ac), jnp.sin(ac)], axis=-1)
    pad = s_len - t_lat
    cos = jnp.concatenate([cos, jnp.ones((pad, HEAD_DIM), F32)], axis=0)
    sin = jnp.concatenate([sin, jnp.zeros((pad, HEAD_DIM), F32)], axis=0)
    return cos, sin


def _pick(n, candidates):
    for c in candidates:
        if n % c == 0:
            return c
    raise ValueError(f"no tile in {candidates} divides {n}")


def kernel(x, c, ctx, c_ctx, w_mod, b_mod, g_norm, w_ffn1_up, w_ffn1_down, w_ffn2_up, w_ffn2_down,
           w_in, q_gain, k_gain, conv_w, conv_b, gate_b, m_gain, w_out, g_final):
    bsz, t_lat, d = x.shape
    t_ctx = ctx.shape[1]
    s_len = t_lat + t_ctx
    depth = w_mod.shape[0]
    assert depth == 1, "single-layer problem: the context stream update is never needed"
    mix = w_out.shape[1]
    att_w = mix // 2
    kvh = att_w // HEAD_DIM // GQA_GROUP
    kv_w = kvh * HEAD_DIM
    m_w = mix - att_w
    mh = m_w // HEAD_DIM
    n_gates = 4 * mh
    assert w_in.shape[2] == att_w + 2 * kv_w + 4 * m_w + n_gates
    assert t_lat % MLSTM_L == 0 and t_ctx % MLSTM_L == 0 and t_lat % GRID_W == 0
    assert t_lat % t_ctx == 0 and t_lat % ROW_BLOCK == 0 and t_ctx % ROW_BLOCK == 0
    assert m_w % (GQA_GROUP * HEAD_DIM) == 0 and n_gates <= LANES

    wi = w_in[0]
    tn = 256 if kv_w % 256 == 0 else HEAD_DIM
    o_mqk = att_w + 2 * kv_w
    o_gt = o_mqk + 4 * m_w
    n_m = 4 * m_w // tn

    def src_tile(j):
        return jnp.where(j < n_m, j + o_mqk // tn, j - n_m)

    n_cols = o_gt
    c_mo, c_aq, c_ak, c_av = 3 * m_w, 4 * m_w, 4 * m_w + att_w, 4 * m_w + att_w + kv_w
    tiles = {"conv": 2 * m_w // tn, "kscale": m_w // tn, "plain": c_aq // tn,
             "q": c_ak // tn, "k": c_av // tn}
    w_gate = jnp.pad(wi[:, o_gt:], ((0, 0), (0, LANES - n_gates))).astype(BF16)
    gate_bias = jnp.pad(gate_b[0], (0, LANES - n_gates))[None, :]
    gains = jnp.stack([q_gain[0] * HEAD_DIM ** -0.5, k_gain[0]])
    cw = jnp.pad(conv_w[0], ((0, SUBLANES - CONV_K), (0, 0)))
    cb = conv_b[0][None, :]
    cos, sin = _rope_tables(t_lat, s_len)
    w1u, w1d = w_ffn1_up[0].astype(BF16), w_ffn1_down[0].astype(BF16)
    w2u, w2d = w_ffn2_up[0].astype(BF16), w_ffn2_down[0].astype(BF16)
    wo = w_out[0].astype(BF16)
    gn = g_norm[0]
    gfin = g_final[None, :]

    rows = -(-(bsz + 1) // SUBLANES) * SUBLANES
    cc = jnp.pad(jnp.concatenate([c, c_ctx[None, :]], axis=0), ((0, rows - bsz - 1), (0, 0)))
    mod = _modulation(cc, w_mod[0], b_mod[0][None, :], _pick(N_MOD * d, (1024, 512, 256, 128)))
    mod_l = mod[:bsz].reshape(bsz, N_MOD, d)
    mod_c = mod[bsz:bsz + 1].reshape(1, N_MOD, d)

    f_dim = w1d.shape[0]
    tf = _pick(f_dim, (512, 256, 128))
    tm = _pick(t_lat, (512, 256, 128))
    tm_c = _pick(bsz * t_ctx, (512, 256, 128))
    assert tm_c % t_ctx == 0
    nb = tm_c // t_ctx
    _, h_c = _ffn(ctx.reshape(1, bsz * t_ctx, d), mod_c, gn, gfin, w1u, w1d, mod_i=0, g_i=0,
                  tm=tm_c, tf=tf, h_shape=(bsz, s_len // t_ctx, t_ctx, d),
                  h_spec=pl.BlockSpec((nb, None, t_ctx, d), lambda b, i, f: (i, t_lat // t_ctx, 0, 0)))
    x1, h_all = _ffn(x, mod_l, gn, gfin, w1u, w1d, mod_i=0, g_i=0, tm=tm, tf=tf,
                     h_shape=(bsz, s_len, d), h_prev=h_c.reshape(bsz, s_len, d),
                     h_spec=pl.BlockSpec((None, tm, d), lambda b, i, f: (b, i, 0)))

    proj, kt = _projection(h_all, wi, cos, sin, gains, cw, cb, t_lat=t_lat, tn=tn, n_cols=n_cols,
                           src_tile=src_tile, tiles=tiles)
    g, ut = _gates(h_all, w_gate, gate_bias, mh=mh, lc=MLSTM_L)
    hm = _mlstm(proj, kt, g, ut, t_lat=t_lat, m_w=m_w, mh=mh, lc=MLSTM_L, v_col=2)
    gw = GQA_GROUP * HEAD_DIM
    att = _attention(proj, t_lat=t_lat, q_col=c_aq // gw, k_col=c_ak // HEAD_DIM,
                     v_col=c_av // HEAD_DIM, kvh=kvh, tq=_pick(t_lat, (512, 256, 128)))
    x2 = _out_projection(att, hm, proj, m_gain, wo, x1, mod_l, mo_col=c_mo // m_w, tm=tm)

    (out,) = _ffn(x2, mod_l, gn, gfin, w2u, w2d, mod_i=6, g_i=2, tm=tm, tf=tf)
    return out
```
